```python
import math, functools
import jax, jax.numpy as jnp
from jax import lax
import numpy as np

D_MODEL = 2048
BATCH = 2
SEQ = 4096
DEPTH = 2
DEC_BATCH = 128
DEC_SEQ = 1
PAST_LEN = 2048
PAGE_SIZE = 128

HEAD_DIM = 128
N_HEADS = D_MODEL // HEAD_DIM
KV_HEADS = 4
Q_PER_KV = N_HEADS // KV_HEADS
ATTN_WIDTH = N_HEADS * HEAD_DIM
KV_WIDTH = KV_HEADS * HEAD_DIM
CMP_BLOCK = 32
SEL_BLOCK = 64
N_SELECT = 16
WINDOW = 512
Q_BLOCK = 128
ROPE_THETA = 10000.0
ATTN_SCALE = HEAD_DIM ** -0.5
FORCE_SCORE = 1e30
NEG_SCORE = -1e30
D_INNER = 2 * D_MODEL
SSM_HEAD_DIM = 64
SSM_HEADS = D_INNER // SSM_HEAD_DIM
SSM_GROUPS = 8
SSM_HPG = SSM_HEADS // SSM_GROUPS
SSM_STATE = 128
CONV_WIDTH = 4
CONV_DIM = D_INNER + 2 * SSM_GROUPS * SSM_STATE
SSM_CHUNK = 128
DT_MIN = 1e-3
DT_MAX = 1e-1
D_FF = -(-8 * D_MODEL // (3 * 256)) * 256
EPS = 1e-6
IN_SPLITS = (D_INNER, CONV_DIM, SSM_HEADS, ATTN_WIDTH, KV_WIDTH, KV_WIDTH, KV_WIDTH, KV_WIDTH,
             KV_WIDTH, KV_WIDTH, 3 * N_HEADS, 2 * D_MODEL)
N_IN = sum(IN_SPLITS)

kernel_name = 'hybrid_ssd_nsa_adaln_decoder_step'


def _rms(x):
    xf = x.astype(jnp.float32)
    return xf * lax.rsqrt(jnp.mean(xf * xf, axis=-1, keepdims=True) + EPS)


def rms_norm(x, g):
    return (_rms(x) * g.astype(jnp.float32)).astype(x.dtype)


def modulate(h, shift, scale):
    return h * (1.0 + scale[:, None]) + shift[:, None]


def split_cols(p, sizes):
    offs = [int(o) for o in np.cumsum(sizes)[:-1]]
    return jnp.split(p, offs, axis=-1)


def masked_softmax(s, mask):
    s = jnp.where(mask, s.astype(jnp.float32), NEG_SCORE)
    m = jnp.max(s, axis=-1, keepdims=True)
    e = jnp.where(mask, jnp.exp(s - m), 0.0)
    return e / jnp.maximum(jnp.sum(e, axis=-1, keepdims=True), 1e-30)


def rope(x, pos):
    half = HEAD_DIM // 2
    inv = ROPE_THETA ** (-jnp.arange(half, dtype=jnp.float32) / half)
    ang = pos.astype(jnp.float32)[:, None] * inv[None, :]
    cos = jnp.cos(ang)[:, None, :]
    sin = jnp.sin(ang)[:, None, :]
    xf = x.astype(jnp.float32)
    x1, x2 = xf[..., :half], xf[..., half:]
    return jnp.concatenate([x1 * cos - x2 * sin, x2 * cos + x1 * sin], axis=-1).astype(x.dtype)


def swiglu(h, w1, w2):
    g, u = jnp.split(h @ w1, 2, axis=-1)
    return (jax.nn.silu(g) * u) @ w2


def causal_conv(xbc, buf, w, bias):
    t = xbc.shape[1]
    xp = jnp.concatenate([buf.astype(xbc.dtype), xbc], axis=1)
    out = bias
    for k in range(CONV_WIDTH):
        out = out + xp[:, k:k + t] * w[k]
    return jax.nn.silu(out), xp[:, t:]


def ssm_inputs(xbc_act, dt_raw, dt_bias, a_log):
    b, t, _ = xbc_act.shape
    xs, bm, cm = split_cols(xbc_act, (D_INNER, SSM_GROUPS * SSM_STATE, SSM_GROUPS * SSM_STATE))
    xh = xs.reshape(b, t, SSM_GROUPS, SSM_HPG, SSM_HEAD_DIM)
    bm = bm.reshape(b, t, SSM_GROUPS, SSM_STATE)
    cm = cm.reshape(b, t, SSM_GROUPS, SSM_STATE)
    dt = jax.nn.softplus(dt_raw.astype(jnp.float32) + dt_bias.astype(jnp.float32))
    dt = dt.reshape(b, t, SSM_GROUPS, SSM_HPG)
    a = -jnp.exp(a_log.astype(jnp.float32)).reshape(SSM_GROUPS, SSM_HPG)
    return xh, bm, cm, dt, a


def ssd_chunked(xh, dt, a, bm, cm):
    b, t = xh.shape[:2]
    c, L = t // SSM_CHUNK, SSM_CHUNK
    x = xh.reshape(b, c, L, SSM_GROUPS, SSM_HPG, SSM_HEAD_DIM)
    dtc = dt.reshape(b, c, L, SSM_GROUPS, SSM_HPG)
    bc = bm.reshape(b, c, L, SSM_GROUPS, SSM_STATE)
    cc = cm.reshape(b, c, L, SSM_GROUPS, SSM_STATE)
    a_cs = jnp.cumsum(jnp.moveaxis(dtc * a, 2, -1), axis=-1)
    xdt = x * dtc[..., None]
    tril = jnp.tril(jnp.ones((L, L), dtype=bool))
    diff = a_cs[..., :, None] - a_cs[..., None, :]
    lmat = jnp.where(tril, jnp.exp(jnp.where(tril, diff, 0.0)), 0.0)
    cb = jnp.einsum('bclgn,bcsgn->bcgls', cc, bc)
    y_diag = jnp.einsum('bcgls,bcgjls,bcsgjp->bclgjp', cb, lmat, xdt)
    decay_states = jnp.exp(a_cs[..., -1:] - a_cs)
    states = jnp.einsum('bclgn,bcgjl,bclgjp->bcgjpn', bc, decay_states, xdt).astype(jnp.float32)
    chunk_decay = jnp.exp(a_cs[..., -1])

    def step(s, inp):
        dec, st = inp
        return dec[..., None, None] * s + st, s

    s0 = jnp.zeros((b, SSM_GROUPS, SSM_HPG, SSM_HEAD_DIM, SSM_STATE), jnp.float32)
    final, prev = lax.scan(step, s0, (jnp.moveaxis(chunk_decay, 1, 0), jnp.moveaxis(states, 1, 0)))
    prev = jnp.moveaxis(prev, 0, 1)
    y_off = jnp.einsum('bclgn,bcgjpn,bcgjl->bclgjp', cc, prev, jnp.exp(a_cs))
    y = (y_diag + y_off).reshape(b, t, SSM_GROUPS, SSM_HPG, SSM_HEAD_DIM)
    return y, final


def ssm_recurrent(xh, dt, a, bm, cm, s0):
    def step(s, inp):
        x_t, dt_t, b_t, c_t = inp
        s = jnp.exp(dt_t * a)[..., None, None] * s + jnp.einsum('bgj,bgjp,bgn->bgjpn', dt_t, x_t, b_t)
        return s, jnp.einsum('bgn,bgjpn->bgjp', c_t, s)

    seqs = (jnp.moveaxis(xh, 1, 0), jnp.moveaxis(dt, 1, 0), jnp.moveaxis(bm, 1, 0), jnp.moveaxis(cm, 1, 0))
    s, ys = lax.scan(step, s0.astype(jnp.float32), seqs)
    return jnp.moveaxis(ys, 0, 1), s


def ssm_finish(y, xh, z, d_skip, g_norm):
    b, t = z.shape[:2]
    y = y + xh * d_skip.reshape(SSM_GROUPS, SSM_HPG)[..., None]
    y = y.reshape(b, t, D_INNER) * jax.nn.silu(z)
    yn = _rms(y.reshape(b, t, SSM_GROUPS, D_INNER // SSM_GROUPS)).reshape(b, t, D_INNER)
    return (yn * g_norm.astype(jnp.float32)).astype(z.dtype)


def nsa_heads(q, kv, pos):
    b, t = q.shape[:2]
    qh = rope(q.reshape(b, t, N_HEADS, HEAD_DIM), pos).reshape(b, t, KV_HEADS, Q_PER_KV, HEAD_DIM)
    k_c, v_c, k_s, v_s, k_w, v_w = [u.reshape(b, t, KV_HEADS, HEAD_DIM) for u in kv]
    return qh, rope(k_c, pos), v_c, rope(k_s, pos), v_s, rope(k_w, pos), v_w


def compress(x_tok, pool):
    b, t = x_tok.shape[:2]
    n = t // CMP_BLOCK
    xb = x_tok[:, :n * CMP_BLOCK].reshape(b, n, CMP_BLOCK, KV_HEADS, HEAD_DIM)
    return jnp.einsum('bnlgd,lgd->bngd', xb, pool)


def cmp_attend(q, kc, vc, qpos):
    n = kc.shape[1]
    s = jnp.einsum('bqgjd,bngd->bqgjn', q, kc) * ATTN_SCALE
    vis = (jnp.arange(n) + 1) * CMP_BLOCK - 1 <= qpos[:, None]
    p = masked_softmax(s, vis[None, :, None, None, :])
    o = jnp.einsum('bqgjn,bngd->bqgjd', p.astype(vc.dtype), vc)
    return o, jnp.sum(p, axis=3)


def select_blocks(imp, qpos, n_sb):
    n_cmp = imp.shape[-1]
    ratio = SEL_BLOCK // CMP_BLOCK
    imp = jnp.pad(imp, ((0, 0), (0, 0), (0, 0), (0, n_sb * ratio - n_cmp)))
    imp = imp.reshape(imp.shape[0], imp.shape[1], imp.shape[2], n_sb, ratio).sum(-1)
    blk = jnp.arange(n_sb)[None, :]
    cur = (qpos // SEL_BLOCK)[:, None]
    forced = (blk == 0) | (blk == cur) | (blk == cur - 1)
    causal = blk * SEL_BLOCK <= qpos[:, None]
    score = jnp.where(forced[None, :, None, :], FORCE_SCORE,
                      jnp.where(causal[None, :, None, :], imp, NEG_SCORE))
    _, idx = lax.top_k(score, min(N_SELECT, n_sb))
    return idx


def to_sel_blocks(x, n_sb):
    b, t = x.shape[:2]
    x = jnp.pad(x, ((0, 0), (0, n_sb * SEL_BLOCK - t), (0, 0), (0, 0)))
    return x.reshape(b, n_sb, SEL_BLOCK, KV_HEADS, HEAD_DIM).transpose(0, 3, 1, 2, 4)


def sel_attend(q, kb, vb, idx, qpos):
    bi = jnp.arange(q.shape[0])[:, None, None, None]
    gi = jnp.arange(KV_HEADS)[None, None, :, None]
    kg = kb[bi, gi, idx]
    vg = vb[bi, gi, idx]
    s = jnp.einsum('bqgjd,bqgnkd->bqgjnk', q, kg) * ATTN_SCALE
    kpos = idx[..., None] * SEL_BLOCK + jnp.arange(SEL_BLOCK)
    mask = (kpos <= qpos[None, :, None, None, None])[:, :, :, None]
    sh = s.shape
    p = masked_softmax(s.reshape(sh[0], sh[1], sh[2], sh[3], -1),
                       mask.reshape(sh[0], sh[1], sh[2], 1, -1)).reshape(sh)
    return jnp.einsum('bqgjnk,bqgnkd->bqgjd', p.astype(vg.dtype), vg)


def window_attend_prompt(q, kw, vw):
    b, t = kw.shape[:2]
    nq = t // Q_BLOCK
    nwb = WINDOW // Q_BLOCK

    def band(x):
        xp = jnp.pad(x, ((0, 0), (WINDOW, 0), (0, 0), (0, 0)))
        xb = xp.reshape(b, nq + nwb, Q_BLOCK, KV_HEADS, HEAD_DIM)
        return jnp.concatenate([xb[:, j:j + nq] for j in range(nwb + 1)], axis=2)

    kb, vb = band(kw), band(vw)
    qb = q.reshape(b, nq, Q_BLOCK, KV_HEADS, Q_PER_KV, HEAD_DIM)
    s = jnp.einsum('bnqgjd,bnkgd->bnqgjk', qb, kb) * ATTN_SCALE
    qpos = jnp.arange(nq)[:, None] * Q_BLOCK + jnp.arange(Q_BLOCK)[None, :]
    kpos = jnp.arange(nq)[:, None] * Q_BLOCK - WINDOW + jnp.arange((nwb + 1) * Q_BLOCK)[None, :]
    diff = qpos[:, :, None] - kpos[:, None, :]
    mask = (diff >= 0) & (diff <= WINDOW) & (kpos[:, None, :] >= 0)
    p = masked_softmax(s, mask[None, :, :, None, None, :])
    o = jnp.einsum('bnqgjk,bnkgd->bnqgjd', p.astype(vb.dtype), vb)
    return o.reshape(b, t, KV_HEADS, Q_PER_KV, HEAD_DIM)


def window_attend_sample(q, kbuf, vbuf, knew, vnew, qpos):
    wb = kbuf.shape[1]
    k = jnp.concatenate([kbuf.astype(knew.dtype), knew], axis=1)
    v = jnp.concatenate([vbuf.astype(vnew.dtype), vnew], axis=1)
    kpos = PAST_LEN - wb + jnp.arange(k.shape[1])
    s = jnp.einsum('bqgjd,bkgd->bqgjk', q, k) * ATTN_SCALE
    diff = qpos[:, None] - kpos[None, :]
    mask = (diff >= 0) & (diff <= WINDOW)
    p = masked_softmax(s, mask[None, :, None, None, :])
    o = jnp.einsum('bqgjk,bkgd->bqgjd', p.astype(v.dtype), v)
    return o, k[:, -wb:], v[:, -wb:]


def nsa_combine(o_c, o_s, o_w, g_nsa):
    b, t = g_nsa.shape[:2]
    g = jax.nn.sigmoid(g_nsa).reshape(b, t, KV_HEADS, Q_PER_KV, 3)
    return g[..., 0:1] * o_c + g[..., 1:2] * o_s + g[..., 2:3] * o_w


def merge_branches(y_ssm, o_attn, g_mrg, lp):
    b, t = y_ssm.shape[:2]
    ys = y_ssm @ lp['w_ssm_br']
    ya = o_attn.reshape(b, t, ATTN_WIDTH) @ lp['w_attn_br']
    gs, ga = jnp.split(jax.nn.sigmoid(g_mrg), 2, axis=-1)
    return (gs * ys + ga * ya) @ lp['w_out']


def gather_pages(cache, l, page_table):
    g = cache[l, page_table]
    return g.reshape(page_table.shape[0], page_table.shape[1] * PAGE_SIZE, KV_HEADS, HEAD_DIM)


def mixer_prompt(h, lp):
    b, t, _ = h.shape
    pos = jnp.arange(t, dtype=jnp.int32)
    z, xbc, dt_raw, q, k_c, v_c, k_s, v_s, k_w, v_w, g_nsa, g_mrg = split_cols(h @ lp['w_in'], IN_SPLITS)
    buf0 = jnp.zeros((b, CONV_WIDTH - 1, CONV_DIM), xbc.dtype)
    xbc_act, conv_state = causal_conv(xbc, buf0, lp['conv_w'], lp['conv_b'])
    xh, bm, cm, dt, a = ssm_inputs(xbc_act, dt_raw, lp['dt_bias'], lp['a_log'])
    y, s_fin = ssd_chunked(xh, dt, a, bm, cm)
    y_ssm = ssm_finish(y, xh, z, lp['d_skip'], lp['g_ssm'])
    q, k_c, v_c, k_s, v_s, k_w, v_w = nsa_heads(q, (k_c, v_c, k_s, v_s, k_w, v_w), pos)
    o_c, imp = cmp_attend(q, compress(k_c, lp['pool_k']), compress(v_c, lp['pool_v']), pos)
    n_sb = -(-t // SEL_BLOCK)
    idx = select_blocks(imp, pos, n_sb)
    kb, vb = to_sel_blocks(k_s, n_sb), to_sel_blocks(v_s, n_sb)
    nq = t // Q_BLOCK
    q_blocks = jnp.moveaxis(q.reshape(b, nq, Q_BLOCK, KV_HEADS, Q_PER_KV, HEAD_DIM), 1, 0)
    i_blocks = jnp.moveaxis(idx.reshape(b, nq, Q_BLOCK, KV_HEADS, idx.shape[-1]), 1, 0)
    p_blocks = pos.reshape(nq, Q_BLOCK)
    o_s = lax.map(lambda args: sel_attend(args[0], kb, vb, args[1], args[2]), (q_blocks, i_blocks, p_blocks))
    o_s = jnp.moveaxis(o_s, 0, 1).reshape(b, t, KV_HEADS, Q_PER_KV, HEAD_DIM)
    o_w = window_attend_prompt(q, k_w, v_w)
    o = nsa_combine(o_c, o_s, o_w, g_nsa)
    out = merge_branches(y_ssm, o, g_mrg, lp)
    wb = min(WINDOW, t)
    state = (k_c, v_c, k_s, v_s, k_w[:, t - wb:], v_w[:, t - wb:],
             s_fin.reshape(b, SSM_HEADS, SSM_HEAD_DIM, SSM_STATE), conv_state)
    return out, state


def mixer_sample(h, lp, l, cache_k_cmp, cache_v_cmp, cache_k_sel, cache_v_sel, cache_k_win, cache_v_win,
                 state_ssm, state_conv, page_table):
    b, t, _ = h.shape
    pos = PAST_LEN + jnp.arange(t, dtype=jnp.int32)
    z, xbc, dt_raw, q, k_c, v_c, k_s, v_s, k_w, v_w, g_nsa, g_mrg = split_cols(h @ lp['w_in'], IN_SPLITS)
    xbc_act, conv_new = causal_conv(xbc, state_conv[l], lp['conv_w'], lp['conv_b'])
    xh, bm, cm, dt, a = ssm_inputs(xbc_act, dt_raw, lp['dt_bias'], lp['a_log'])
    s0 = state_ssm[l].reshape(b, SSM_GROUPS, SSM_HPG, SSM_HEAD_DIM, SSM_STATE)
    y, s_new = ssm_recurrent(xh, dt, a, bm, cm, s0)
    y_ssm = ssm_finish(y, xh, z, lp['d_skip'], lp['g_ssm'])
    q, k_c, v_c, k_s, v_s, k_w, v_w = nsa_heads(q, (k_c, v_c, k_s, v_s, k_w, v_w), pos)
    kc_all = jnp.concatenate([gather_pages(cache_k_cmp, l, page_table).astype(k_c.dtype), k_c], axis=1)
    vc_all = jnp.concatenate([gather_pages(cache_v_cmp, l, page_table).astype(v_c.dtype), v_c], axis=1)
    ks_all = jnp.concatenate([gather_pages(cache_k_sel, l, page_table).astype(k_s.dtype), k_s], axis=1)
    vs_all = jnp.concatenate([gather_pages(cache_v_sel, l, page_table).astype(v_s.dtype), v_s], axis=1)
    o_c, imp = cmp_attend(q, compress(kc_all, lp['pool_k']), compress(vc_all, lp['pool_v']), pos)
    n_sb = -(-kc_all.shape[1] // SEL_BLOCK)
    idx = select_blocks(imp, pos, n_sb)
    o_s = sel_attend(q, to_sel_blocks(ks_all, n_sb), to_sel_blocks(vs_all, n_sb), idx, pos)
    o_w, kw_buf, vw_buf = window_attend_sample(q, cache_k_win[l], cache_v_win[l], k_w, v_w, pos)
    o = nsa_combine(o_c, o_s, o_w, g_nsa)
    out = merge_branches(y_ssm, o, g_mrg, lp)
    state = (k_c, v_c, k_s, v_s, kw_buf, vw_buf,
             s_new.reshape(b, SSM_HEADS, SSM_HEAD_DIM, SSM_STATE), conv_new)
    return out, state


def apply_layer(x, c, mixer_fn, ada_w, ada_b, g_mix, g_ffn, w_f1, w_f2):
    mod = jax.nn.silu(c) @ ada_w + ada_b
    sh_a, sc_a, gt_a, sh_f, sc_f, gt_f = jnp.split(mod, 6, axis=-1)
    mo, state = mixer_fn(modulate(rms_norm(x, g_mix), sh_a, sc_a))
    x = x + gt_a[:, None] * mo
    f = swiglu(modulate(rms_norm(x, g_ffn), sh_f, sc_f), w_f1, w_f2)
    return x + gt_f[:, None] * f, state


def setup_inputs(seed: int = 0) -> dict:
    key = jax.random.key(seed)
    ks = jax.random.split(key, 32)
    f32 = jnp.float32
    n_pages = PAST_LEN // PAGE_SIZE
    n_pool = DEC_BATCH * n_pages + (DEC_BATCH * n_pages + 3) // 4
    wbuf = min(WINDOW, PAST_LEN)

    def nrm(k, shape, s=1.0):
        return s * jax.random.normal(k, shape, f32)

    kv_pool = (DEPTH, n_pool, PAGE_SIZE, KV_HEADS, HEAD_DIM)
    kv_win = (DEPTH, DEC_BATCH, wbuf, KV_HEADS, HEAD_DIM)
    page_table = jax.random.permutation(ks[10], n_pool)[:DEC_BATCH * n_pages]
    page_table = page_table.reshape(DEC_BATCH, n_pages).astype(jnp.int32)
    u = jax.random.uniform(ks[20], (DEPTH, SSM_HEADS), f32)
    dt0 = jnp.exp(u * (math.log(DT_MAX) - math.log(DT_MIN)) + math.log(DT_MIN))
    dt_bias = dt0 + jnp.log(-jnp.expm1(-dt0))
    a_log = jnp.log(jax.random.uniform(ks[21], (DEPTH, SSM_HEADS), f32, minval=1.0, maxval=16.0))
    return {
        'x_prompt': nrm(ks[0], (BATCH, SEQ, D_MODEL)),
        'x_sample': nrm(ks[1], (DEC_BATCH, DEC_SEQ, D_MODEL)),
        'cache_k_cmp': nrm(ks[2], kv_pool),
        'cache_v_cmp': nrm(ks[3], kv_pool),
        'cache_k_sel': nrm(ks[4], kv_pool),
        'cache_v_sel': nrm(ks[5], kv_pool),
        'cache_k_win': nrm(ks[6], kv_win),
        'cache_v_win': nrm(ks[7], kv_win),
        'state_ssm': nrm(ks[8], (DEPTH, DEC_BATCH, SSM_HEADS, SSM_HEAD_DIM, SSM_STATE), 0.5),
        'state_conv': nrm(ks[9], (DEPTH, DEC_BATCH, CONV_WIDTH - 1, CONV_DIM)),
        'page_table': page_table,
        'c_prompt': nrm(ks[11], (BATCH, D_MODEL)),
        'c_sample': nrm(ks[12], (DEC_BATCH, D_MODEL)),
        'w_ada': nrm(ks[13], (DEPTH, D_MODEL, 6 * D_MODEL), 0.5 * D_MODEL ** -0.5),
        'b_ada': nrm(ks[14], (DEPTH, 6 * D_MODEL), 0.02),
        'g_norm_mix': 1.0 + nrm(ks[15], (DEPTH, D_MODEL), 0.02),
        'g_norm_ffn': 1.0 + nrm(ks[16], (DEPTH, D_MODEL), 0.02),
        'g_norm_final': 1.0 + nrm(ks[17], (D_MODEL,), 0.02),
        'w_in': nrm(ks[18], (DEPTH, D_MODEL, N_IN), D_MODEL ** -0.5),
        'conv_w': nrm(ks[19], (DEPTH, CONV_WIDTH, CONV_DIM), CONV_WIDTH ** -0.5),
        'conv_b': nrm(ks[22], (DEPTH, CONV_DIM), 0.02),
        'dt_bias': dt_bias,
        'a_log': a_log,
        'd_skip': 1.0 + nrm(ks[23], (DEPTH, SSM_HEADS), 0.1),
        'g_ssm': 1.0 + nrm(ks[24], (DEPTH, D_INNER), 0.02),
        'pool_k': nrm(ks[25], (DEPTH, CMP_BLOCK, KV_HEADS, HEAD_DIM), CMP_BLOCK ** -0.5),
        'pool_v': nrm(ks[26], (DEPTH, CMP_BLOCK, KV_HEADS, HEAD_DIM), CMP_BLOCK ** -0.5),
        'w_ssm_br': nrm(ks[27], (DEPTH, D_INNER, D_MODEL), D_INNER ** -0.5),
        'w_attn_br': nrm(ks[28], (DEPTH, ATTN_WIDTH, D_MODEL), ATTN_WIDTH ** -0.5),
        'w_out': nrm(ks[29], (DEPTH, D_MODEL, D_MODEL), D_MODEL ** -0.5),
        'w_ffn_in': nrm(ks[30], (DEPTH, D_MODEL, 2 * D_FF), D_MODEL ** -0.5),
        'w_ffn_out': nrm(ks[31], (DEPTH, D_FF, D_MODEL), D_FF ** -0.5),
    }


def reference(x_prompt, x_sample, cache_k_cmp, cache_v_cmp, cache_k_sel, cache_v_sel, cache_k_win, cache_v_win,
              state_ssm, state_conv, page_table, c_prompt, c_sample, w_ada, b_ada, g_norm_mix, g_norm_ffn,
              g_norm_final, w_in, conv_w, conv_b, dt_bias, a_log, d_skip, g_ssm, pool_k, pool_v, w_ssm_br,
              w_attn_br, w_out, w_ffn_in, w_ffn_out):
    xp, xs = x_prompt, x_sample
    st_p, st_s = [], []
    for l in range(DEPTH):
        lp = {'w_in': w_in[l], 'conv_w': conv_w[l], 'conv_b': conv_b[l], 'dt_bias': dt_bias[l],
              'a_log': a_log[l], 'd_skip': d_skip[l], 'g_ssm': g_ssm[l], 'pool_k': pool_k[l],
              'pool_v': pool_v[l], 'w_ssm_br': w_ssm_br[l], 'w_attn_br': w_attn_br[l], 'w_out': w_out[l]}
        common = (w_ada[l], b_ada[l], g_norm_mix[l], g_norm_ffn[l], w_ffn_in[l], w_ffn_out[l])
        xp, sp = apply_layer(xp, c_prompt, functools.partial(mixer_prompt, lp=lp), *common)
        mix_s = functools.partial(mixer_sample, lp=lp, l=l, cache_k_cmp=cache_k_cmp, cache_v_cmp=cache_v_cmp,
                                  cache_k_sel=cache_k_sel, cache_v_sel=cache_v_sel, cache_k_win=cache_k_win,
                                  cache_v_win=cache_v_win, state_ssm=state_ssm, state_conv=state_conv,
                                  page_table=page_table)
        xs, ss = apply_layer(xs, c_sample, mix_s, *common)
        st_p.append(sp)
        st_s.append(ss)
    y_prompt = rms_norm(xp, g_norm_final)
    y_sample = rms_norm(xs, g_norm_final)
    k_cmp_p, v_cmp_p, k_sel_p, v_sel_p, k_win_p, v_win_p, ssm_p, conv_p = [
        jnp.stack([s[i] for s in st_p]) for i in range(8)]
    k_cmp_s, v_cmp_s, k_sel_s, v_sel_s, k_win_s, v_win_s, ssm_s, conv_s = [
        jnp.stack([s[i] for s in st_s]) for i in range(8)]
    return (y_prompt, y_sample, k_cmp_p, v_cmp_p, k_sel_p, v_sel_p, k_win_p, v_win_p, ssm_p, conv_p,
            k_cmp_s, v_cmp_s, k_sel_s, v_sel_s, k_win_s, v_win_s, ssm_s, conv_s)
```

```python
import functools

import jax
import jax.numpy as jnp
from jax import lax
from jax.experimental import pallas as pl
from jax.experimental.pallas import tpu as pltpu

F32 = jnp.float32
BF16 = jnp.bfloat16

D_MODEL = 2048
DEPTH = 2
PAGE_SIZE = 128
HEAD_DIM = 128
N_HEADS = D_MODEL // HEAD_DIM
KV_HEADS = 4
Q_PER_KV = N_HEADS // KV_HEADS
ATTN_WIDTH = N_HEADS * HEAD_DIM
KV_WIDTH = KV_HEADS * HEAD_DIM
CMP_BLOCK = 32
SEL_BLOCK = 64
SEL_SHIFT = SEL_BLOCK.bit_length() - 1
assert SEL_BLOCK == 2 * CMP_BLOCK
N_SELECT = 16
WINDOW = 512
ROPE_THETA = 10000.0
ATTN_SCALE = HEAD_DIM ** -0.5
FORCE_SCORE = 1e30
NEG_SCORE = -1e30
D_INNER = 2 * D_MODEL
SSM_HEAD_DIM = 64
SSM_HEADS = D_INNER // SSM_HEAD_DIM
SSM_GROUPS = 8
SSM_HPG = SSM_HEADS // SSM_GROUPS
SSM_STATE = 128
CONV_WIDTH = 4
BC_WIDTH = 2 * SSM_GROUPS * SSM_STATE
CONV_DIM = D_INNER + BC_WIDTH
SSM_CHUNK = 128
D_FF = -(-8 * D_MODEL // (3 * 256)) * 256
EPS = 1e-6
GROUP_NORM = D_INNER // SSM_GROUPS

LANES = 128
SUBLANES = 8
VMEM_LIMIT_BYTES = 52 * 1024 * 1024

C_X = 0
C_Z = C_X + D_INNER
C_GM = C_Z + D_INNER
C_BC = C_GM + 2 * D_MODEL
C_Q = C_BC + BC_WIDTH
C_KV = C_Q + ATTN_WIDTH
C_MISC = C_KV + 6 * KV_WIDTH
MISC_WIDTH = 256
N_PROJ = C_MISC + MISC_WIDTH
GN_OFF = SSM_HEADS
PROJ_TN = 1792


def _cparams(*sem):
    return pltpu.CompilerParams(dimension_semantics=sem, vmem_limit_bytes=VMEM_LIMIT_BYTES)


def _dot(a, b):
    return jnp.dot(a, b, preferred_element_type=F32)


def _dot_nt(a, b):
    return lax.dot_general(a, b, (((1,), (1,)), ((), ())), preferred_element_type=F32)


def _dot_exact(a, b):
    return jnp.dot(a, b, preferred_element_type=F32, precision=lax.Precision.HIGHEST)


def _sigmoid(x):
    return 1.0 / (1.0 + jnp.exp(-x))


def _silu(x):
    return x * _sigmoid(x)


def _softplus(x):
    return jnp.maximum(x, 0.0) + jnp.log1p(jnp.exp(-jnp.abs(x)))


def _masked_softmax(s, mask):
    s = jnp.where(mask, s, NEG_SCORE)
    m = jnp.max(s, axis=-1, keepdims=True)
    e = jnp.where(mask, jnp.exp(s - m), 0.0)
    return e / jnp.maximum(jnp.sum(e, axis=-1, keepdims=True), 1e-30)


def _rope(x, cosf, sinf):
    return x * cosf + pltpu.roll(x, HEAD_DIM // 2, 1) * sinf


def _ada_kernel(c_ref, w_ref, b_ref, o_ref):
    o_ref[...] = _dot(_silu(c_ref[...]).astype(BF16), w_ref[...].astype(BF16)) + b_ref[...]


def ada_modulation(c_all, w_ada, b_ada):
    r = c_all.shape[0]
    tn = 1024
    n = 6 * D_MODEL
    return pl.pallas_call(
        _ada_kernel,
        out_shape=jax.ShapeDtypeStruct((DEPTH, r, n), F32),
        grid=(DEPTH, n // tn),
        in_specs=[pl.BlockSpec((r, D_MODEL), lambda l, j: (0, 0)),
                  pl.BlockSpec((None, D_MODEL, tn), lambda l, j: (l, 0, j)),
                  pl.BlockSpec((None, 1, tn), lambda l, j: (l, 0, j))],
        out_specs=pl.BlockSpec((None, r, tn), lambda l, j: (l, 0, j)),
        compiler_params=_cparams("parallel", "parallel"),
        name="ada_modulation",
    )(c_all, w_ada, b_ada.reshape(DEPTH, 1, n))


def _norm_mod_kernel(x_ref, g_ref, sc_ref, sh_ref, o_ref):
    x = x_ref[...]
    xn = x * lax.rsqrt(jnp.mean(x * x, axis=-1, keepdims=True) + EPS) * g_ref[...]
    o_ref[...] = (xn * (1.0 + sc_ref[...]) + sh_ref[...]).astype(o_ref.dtype)


def _mod_spec(mod, chunk, tm, tn, rows_per_group):
    cb = D_MODEL // tn
    if rows_per_group:
        tiles = rows_per_group // tm
        return pl.BlockSpec((None, 1, tn), lambda i, j: (i // tiles, 0, chunk * cb + j))
    return pl.BlockSpec((tm, tn), lambda i, j: (i, chunk * cb + j))


def norm_modulate(x, g, mod, sc_chunk, sh_chunk, rows_per_group, tm):
    m = x.shape[0]
    return pl.pallas_call(
        _norm_mod_kernel,
        out_shape=jax.ShapeDtypeStruct((m, D_MODEL), BF16),
        grid=(m // tm, 1),
        in_specs=[pl.BlockSpec((tm, D_MODEL), lambda i, j: (i, 0)),
                  pl.BlockSpec((1, D_MODEL), lambda i, j: (0, 0)),
                  _mod_spec(mod, sc_chunk, tm, D_MODEL, rows_per_group),
                  _mod_spec(mod, sh_chunk, tm, D_MODEL, rows_per_group)],
        out_specs=pl.BlockSpec((tm, D_MODEL), lambda i, j: (i, 0)),
        compiler_params=_cparams("parallel", "arbitrary"),
        name="norm_modulate",
    )(x, g.reshape(1, D_MODEL), mod, mod)


def _final_norm_kernel(x_ref, g_ref, o_ref):
    x = x_ref[...]
    o_ref[...] = x * lax.rsqrt(jnp.mean(x * x, axis=-1, keepdims=True) + EPS) * g_ref[...]


def final_norm(x, g, tm):
    m = x.shape[0]
    return pl.pallas_call(
        _final_norm_kernel,
        out_shape=jax.ShapeDtypeStruct((m, D_MODEL), F32),
        grid=(m // tm,),
        in_specs=[pl.BlockSpec((tm, D_MODEL), lambda i: (i, 0)),
                  pl.BlockSpec((1, D_MODEL), lambda i: (0, 0))],
        out_specs=pl.BlockSpec((tm, D_MODEL), lambda i: (i, 0)),
        compiler_params=_cparams("parallel"),
        name="final_norm",
    )(x, g.reshape(1, D_MODEL))


def _proj_kernel(h_ref, w_ref, o_ref):
    o_ref[...] = _dot(h_ref[...], w_ref[...])


def in_projection(h, w, tm):
    m = h.shape[0]
    tn = PROJ_TN
    return pl.pallas_call(
        _proj_kernel,
        out_shape=jax.ShapeDtypeStruct((m, N_PROJ), F32),
        grid=(N_PROJ // tn, m // tm),
        in_specs=[pl.BlockSpec((tm, D_MODEL), lambda j, i: (i, 0)),
                  pl.BlockSpec((D_MODEL, tn), lambda j, i: (0, j))],
        out_specs=pl.BlockSpec((tm, tn), lambda j, i: (i, j)),
        compiler_params=_cparams("parallel", "parallel"),
        name="in_projection",
    )(h, w)


def _merge_kernel(ys_ref, oa_ref, ws_ref, wa_ref, gs_ref, ga_ref, o_ref):
    ys = _dot(ys_ref[...], ws_ref[...])
    ya = _dot(oa_ref[...], wa_ref[...])
    o_ref[...] = (_sigmoid(gs_ref[...]) * ys + _sigmoid(ga_ref[...]) * ya).astype(o_ref.dtype)


def merge_branches(y_ssm, o_attn, w_s, w_a, proj, tm):
    m = y_ssm.shape[0]
    tn = 512
    gsb, gab = C_GM // tn, (C_GM + D_MODEL) // tn
    return pl.pallas_call(
        _merge_kernel,
        out_shape=jax.ShapeDtypeStruct((m, D_MODEL), BF16),
        grid=(D_MODEL // tn, m // tm),
        in_specs=[pl.BlockSpec((tm, D_INNER), lambda j, i: (i, 0)),
                  pl.BlockSpec((tm, ATTN_WIDTH), lambda j, i: (i, 0)),
                  pl.BlockSpec((D_INNER, tn), lambda j, i: (0, j)),
                  pl.BlockSpec((ATTN_WIDTH, tn), lambda j, i: (0, j)),
                  pl.BlockSpec((tm, tn), lambda j, i: (i, gsb + j)),
                  pl.BlockSpec((tm, tn), lambda j, i: (i, gab + j))],
        out_specs=pl.BlockSpec((tm, tn), lambda j, i: (i, j)),
        compiler_params=_cparams("parallel", "parallel"),
        name="merge_branches",
    )(y_ssm, o_attn, w_s, w_a, proj, proj)


def _residual_kernel(a_ref, w_ref, x_ref, g_ref, o_ref):
    o_ref[...] = x_ref[...] + g_ref[...] * _dot(a_ref[...], w_ref[...])


def residual_matmul(a, w, x, mod, gate_chunk, rows_per_group, tm, tn):
    m, k = a.shape
    swap = lambda f: (lambda j, i: f(i, j))
    gspec = _mod_spec(mod, gate_chunk, tm, tn, rows_per_group)
    gspec = pl.BlockSpec(gspec.block_shape, swap(gspec.index_map))
    return pl.pallas_call(
        _residual_kernel,
        out_shape=jax.ShapeDtypeStruct((m, D_MODEL), F32),
        grid=(D_MODEL // tn, m // tm),
        in_specs=[pl.BlockSpec((tm, k), lambda j, i: (i, 0)),
                  pl.BlockSpec((k, tn), lambda j, i: (0, j)),
                  pl.BlockSpec((tm, tn), lambda j, i: (i, j)),
                  gspec],
        out_specs=pl.BlockSpec((tm, tn), lambda j, i: (i, j)),
        compiler_params=_cparams("parallel", "parallel"),
        name="residual_matmul",
    )(a, w, x, mod)


def _glu_kernel(h_ref, wg_ref, wu_ref, o_ref):
    h = h_ref[...]
    o_ref[...] = (_silu(_dot(h, wg_ref[...])) * _dot(h, wu_ref[...])).astype(o_ref.dtype)


def glu_matmul(h, w1, tm):
    m = h.shape[0]
    tn = 512
    ub = D_FF // tn
    return pl.pallas_call(
        _glu_kernel,
        out_shape=jax.ShapeDtypeStruct((m, D_FF), BF16),
        grid=(D_FF // tn, m // tm),
        in_specs=[pl.BlockSpec((tm, D_MODEL), lambda j, i: (i, 0)),
                  pl.BlockSpec((D_MODEL, tn), lambda j, i: (0, j)),
                  pl.BlockSpec((D_MODEL, tn), lambda j, i: (0, ub + j))],
        out_specs=pl.BlockSpec((tm, tn), lambda j, i: (i, j)),
        compiler_params=_cparams("parallel", "parallel"),
        name="glu_matmul",
    )(h, w1, w1)


def _ssm_finish(y, xact, z, dskip, gssm):
    y = (y + xact * dskip) * _silu(z)
    outs = []
    for g in range(SSM_GROUPS):
        yg = y[:, g * GROUP_NORM:(g + 1) * GROUP_NORM]
        outs.append(yg * lax.rsqrt(jnp.mean(yg * yg, axis=-1, keepdims=True) + EPS))
    return jnp.concatenate(outs, axis=-1) * gssm


def _ssd_prompt_kernel(x_ref, z_ref, bc_ref, misc_ref, cwx_ref, cbx_ref, cwbc_ref, cbbc_ref, dtb_ref, alog_ref,
                       dskip_ref, gssm_ref,
                       y_ref, st_ref, tailx_ref, tailbc_ref,
                       xbuf, bcbuf, s_scr, yt_scr):
    c = pl.program_id(1)
    L = SSM_CHUNK
    P = SSM_HEAD_DIM
    T8 = SUBLANES

    @pl.when(c == 0)
    def _():
        xbuf[0:T8, :] = jnp.zeros((T8, D_INNER), F32)
        bcbuf[0:T8, :] = jnp.zeros((T8, BC_WIDTH), F32)
        s_scr[...] = jnp.zeros_like(s_scr)

    xbuf[T8:T8 + L, :] = x_ref[...]
    bcbuf[T8:T8 + L, :] = bc_ref[...]

    def conv(buf, w_ref, b_ref):
        acc = b_ref[...] + buf[T8:T8 + L, :] * w_ref[CONV_WIDTH - 1:CONV_WIDTH, :]
        for k in range(CONV_WIDTH - 1):
            lo = T8 - (CONV_WIDTH - 1) + k
            acc = acc + buf[lo:lo + L, :] * w_ref[k:k + 1, :]
        return _silu(acc)

    xact = conv(xbuf, cwx_ref, cbx_ref)
    bcact = conv(bcbuf, cwbc_ref, cbbc_ref)
    xbuf[0:T8, :] = xbuf[L:L + T8, :]
    bcbuf[0:T8, :] = bcbuf[L:L + T8, :]
    tailx_ref[...] = x_ref[L - T8:L, :]
    tailbc_ref[...] = bc_ref[L - T8:L, :]

    lane = lax.broadcasted_iota(jnp.int32, (1, LANES), 1)
    dt = _softplus(misc_ref[...] + dtb_ref[...])
    a = jnp.where(lane < SSM_HEADS, -jnp.exp(alog_ref[...]), 0.0)
    row = lax.broadcasted_iota(jnp.int32, (L, L), 0)
    col = lax.broadcasted_iota(jnp.int32, (L, L), 1)
    tril = row >= col
    a_cs = _dot_exact(tril.astype(F32), dt * a)
    a_cs_t = a_cs.T
    dt_t = dt.T
    a_last = a_cs[L - 1:L, :]
    w_state = jnp.exp(a_last - a_cs) * dt
    e_acs_t = jnp.exp(a_cs_t)
    a_last_t = a_cs_t[:, L - 1:L]

    xt = xact.T.astype(BF16)
    for g in range(SSM_GROUPS):
        bg = bcact[:, g * SSM_STATE:(g + 1) * SSM_STATE]
        cg = bcact[:, (SSM_GROUPS + g) * SSM_STATE:(SSM_GROUPS + g + 1) * SSM_STATE]
        cgb = cg.astype(BF16)
        cb = _dot_nt(cgb, bg.astype(BF16))
        r0 = g * SSM_HPG * P
        s_g = s_scr[r0:r0 + SSM_HPG * P, :]
        y_off_t = _dot_nt(s_g.astype(BF16), cgb)
        for j in range(SSM_HPG):
            h = g * SSM_HPG + j
            rows = slice(h * P, (h + 1) * P)
            diff = a_cs[:, h:h + 1] - a_cs_t[h:h + 1, :]
            mh = jnp.where(tril, jnp.exp(jnp.where(tril, diff, 0.0)), 0.0) * cb * dt_t[h:h + 1, :]
            y_t = _dot_nt(xt[rows, :], mh.astype(BF16))
            y_t = y_t + y_off_t[j * P:(j + 1) * P, :] * e_acs_t[h:h + 1, :]
            yt_scr[rows, :] = y_t
            bw = bg * w_state[:, h:h + 1]
            st = _dot(xt[rows, :], bw.astype(BF16))
            s_scr[rows, :] = jnp.exp(a_last_t[h:h + 1, :]) * s_scr[rows, :] + st

    y = yt_scr[...].T
    y_ref[...] = _ssm_finish(y, xact, z_ref[...], dskip_ref[...], gssm_ref[...]).astype(y_ref.dtype)

    @pl.when(c == pl.num_programs(1) - 1)
    def _():
        st_ref[...] = s_scr[...]


def ssd_prompt(proj, b, t, cwx, cbx, cwbc, cbbc, dtb, alog, dskip, gssm):
    nc = t // SSM_CHUNK
    L = SSM_CHUNK
    full = lambda w: pl.BlockSpec((w.shape[0], w.shape[1]), lambda bi, c: (0, 0))
    y, st, tailx, tailbc = pl.pallas_call(
        _ssd_prompt_kernel,
        out_shape=(jax.ShapeDtypeStruct((b * t, D_INNER), BF16),
                   jax.ShapeDtypeStruct((b, SSM_HEADS * SSM_HEAD_DIM, SSM_STATE), F32),
                   jax.ShapeDtypeStruct((b, SUBLANES, D_INNER), F32),
                   jax.ShapeDtypeStruct((b, SUBLANES, BC_WIDTH), F32)),
        grid=(b, nc),
        in_specs=[pl.BlockSpec((L, D_INNER), lambda bi, c: (bi * nc + c, C_X // D_INNER)),
                  pl.BlockSpec((L, D_INNER), lambda bi, c: (bi * nc + c, C_Z // D_INNER)),
                  pl.BlockSpec((L, BC_WIDTH), lambda bi, c: (bi * nc + c, C_BC // BC_WIDTH)),
                  pl.BlockSpec((L, LANES), lambda bi, c: (bi * nc + c, C_MISC // LANES)),
                  full(cwx), full(cbx), full(cwbc), full(cbbc), full(dtb), full(alog), full(dskip), full(gssm)],
        out_specs=(pl.BlockSpec((L, D_INNER), lambda bi, c: (bi * nc + c, 0)),
                   pl.BlockSpec((None, SSM_HEADS * SSM_HEAD_DIM, SSM_STATE), lambda bi, c: (bi, 0, 0)),
                   pl.BlockSpec((None, SUBLANES, D_INNER), lambda bi, c: (bi, 0, 0)),
                   pl.BlockSpec((None, SUBLANES, BC_WIDTH), lambda bi, c: (bi, 0, 0))),
        scratch_shapes=[pltpu.VMEM((L + 2 * SUBLANES, D_INNER), F32),
                        pltpu.VMEM((L + 2 * SUBLANES, BC_WIDTH), F32),
                        pltpu.VMEM((SSM_HEADS * SSM_HEAD_DIM, SSM_STATE), F32),
                        pltpu.VMEM((D_INNER, L), F32)],
        compiler_params=_cparams("parallel", "arbitrary"),
        name="ssd_prompt",
    )(proj, proj, proj, proj, cwx, cbx, cwbc, cbbc, dtb, alog, dskip, gssm)
    return y, st, tailx, tailbc


def _ssd_sample_pre_kernel(x_ref, bc_ref, misc_ref, stx_ref, stbc_ref, cwx_ref, cbx_ref, cwbc_ref, cbbc_ref,
                           dtb_ref, alog_ref, xact_ref, bcact_ref, dt_ref, da_ref):
    def conv(new, st_ref, w_ref, b_ref):
        acc = b_ref[...] + new * w_ref[CONV_WIDTH - 1:CONV_WIDTH, :]
        for k in range(CONV_WIDTH - 1):
            acc = acc + st_ref[k] * w_ref[k:k + 1, :]
        return _silu(acc)

    xact_ref[...] = conv(x_ref[...], stx_ref, cwx_ref, cbx_ref)
    bcact_ref[...] = conv(bc_ref[...], stbc_ref, cwbc_ref, cbbc_ref)
    lane = lax.broadcasted_iota(jnp.int32, (1, LANES), 1)
    dt = _softplus(misc_ref[...] + dtb_ref[...])
    a = jnp.where(lane < SSM_HEADS, -jnp.exp(alog_ref[...]), 0.0)
    dt_ref[...] = dt
    da_ref[...] = jnp.exp(dt * a)


def ssd_sample_pre(proj, stx, stbc, cwx, cbx, cwbc, cbbc, dtb, alog):
    db = proj.shape[0]
    full2 = lambda w: pl.BlockSpec(w.shape, lambda i: (0, 0))
    full3 = lambda w: pl.BlockSpec(w.shape, lambda i: (0, 0, 0))
    return pl.pallas_call(
        _ssd_sample_pre_kernel,
        out_shape=(jax.ShapeDtypeStruct((db, D_INNER), F32), jax.ShapeDtypeStruct((db, BC_WIDTH), F32),
                   jax.ShapeDtypeStruct((db, LANES), F32), jax.ShapeDtypeStruct((db, LANES), F32)),
        grid=(1,),
        in_specs=[pl.BlockSpec((db, D_INNER), lambda i: (0, C_X // D_INNER)),
                  pl.BlockSpec((db, BC_WIDTH), lambda i: (0, C_BC // BC_WIDTH)),
                  pl.BlockSpec((db, LANES), lambda i: (0, C_MISC // LANES)),
                  full3(stx), full3(stbc), full2(cwx), full2(cbx), full2(cwbc), full2(cbbc), full2(dtb), full2(alog)],
        out_specs=(pl.BlockSpec((db, D_INNER), lambda i: (0, 0)), pl.BlockSpec((db, BC_WIDTH), lambda i: (0, 0)),
                   pl.BlockSpec((db, LANES), lambda i: (0, 0)), pl.BlockSpec((db, LANES), lambda i: (0, 0))),
        compiler_params=_cparams("arbitrary"),
        name="ssd_sample_pre",
    )(proj, proj, proj, stx, stbc, cwx, cbx, cwbc, cbbc, dtb, alog)


def _ssd_sample_step_kernel(xt_ref, dt_ref, da_ref, b_ref, c_ref, s_ref, so_ref, yt_ref):
    P = SSM_HEAD_DIM
    coef_t = xt_ref[...] * dt_ref[...]
    da_t = jnp.broadcast_to(da_ref[...], (P, SSM_HEADS))
    lane = lax.broadcasted_iota(jnp.int32, (P, SSM_HEADS), 1)
    yt = jnp.zeros((P, SSM_HEADS), F32)
    for h in range(SSM_HEADS):
        g = h // SSM_HPG
        new = s_ref[h] * da_t[:, h:h + 1] + coef_t[:, h:h + 1] * b_ref[g:g + 1, :]
        so_ref[h] = new
        yh = jnp.sum(new * c_ref[g:g + 1, :], axis=-1, keepdims=True)
        yt = jnp.where(lane == h, yh, yt)
    yt_ref[...] = yt


def ssd_sample_step(xt, dt3, da3, b3, c3, state, layer):
    db = xt.shape[0]
    return pl.pallas_call(
        _ssd_sample_step_kernel,
        out_shape=(jax.ShapeDtypeStruct((db, SSM_HEADS, SSM_HEAD_DIM, SSM_STATE), F32),
                   jax.ShapeDtypeStruct((db, SSM_HEAD_DIM, SSM_HEADS), F32)),
        grid=(db,),
        in_specs=[pl.BlockSpec((None, SSM_HEAD_DIM, SSM_HEADS), lambda i: (i, 0, 0)),
                  pl.BlockSpec((None, 1, SSM_HEADS), lambda i: (i, 0, 0)),
                  pl.BlockSpec((None, 1, SSM_HEADS), lambda i: (i, 0, 0)),
                  pl.BlockSpec((None, SSM_GROUPS, SSM_STATE), lambda i: (i, 0, 0)),
                  pl.BlockSpec((None, SSM_GROUPS, SSM_STATE), lambda i: (i, 0, 0)),
                  pl.BlockSpec((None, None, SSM_HEADS, SSM_HEAD_DIM, SSM_STATE), lambda i: (layer, i, 0, 0, 0))],
        out_specs=(pl.BlockSpec((None, SSM_HEADS, SSM_HEAD_DIM, SSM_STATE), lambda i: (i, 0, 0, 0)),
                   pl.BlockSpec((None, SSM_HEAD_DIM, SSM_HEADS), lambda i: (i, 0, 0))),
        compiler_params=_cparams("parallel"),
        name="ssd_sample_step",
    )(xt, dt3, da3, b3, c3, state)


def _ssd_sample_finish_kernel(y_ref, x_ref, z_ref, dskip_ref, gssm_ref, o_ref):
    o_ref[...] = _ssm_finish(y_ref[...], x_ref[...], z_ref[...], dskip_ref[...], gssm_ref[...]).astype(o_ref.dtype)


def ssd_sample_finish(y, xact, proj, dskip, gssm):
    db = y.shape[0]
    return pl.pallas_call(
        _ssd_sample_finish_kernel,
        out_shape=jax.ShapeDtypeStruct((db, D_INNER), BF16),
        grid=(1,),
        in_specs=[pl.BlockSpec((db, D_INNER), lambda i: (0, 0)),
                  pl.BlockSpec((db, D_INNER), lambda i: (0, 0)),
                  pl.BlockSpec((db, D_INNER), lambda i: (0, C_Z // D_INNER)),
                  pl.BlockSpec((1, D_INNER), lambda i: (0, 0)),
                  pl.BlockSpec((1, D_INNER), lambda i: (0, 0))],
        out_specs=pl.BlockSpec((db, D_INNER), lambda i: (0, 0)),
        compiler_params=_cparams("arbitrary"),
        name="ssd_sample_finish",
    )(y, xact, proj, dskip, gssm)


def _select_blocks(imp, qpos):
    r = imp.shape[0]
    lane = lax.broadcasted_iota(jnp.int32, (r, LANES), 1)
    imp2 = imp + pltpu.roll(imp, LANES - 1, 1)
    blk = lane >> 1
    even = (lane & 1) == 0
    cur = qpos >> SEL_SHIFT
    forced = (blk == 0) | (blk == cur) | (blk == cur - 1)
    causal = blk * SEL_BLOCK <= qpos
    score = jnp.where(forced, FORCE_SCORE, jnp.where(causal, imp2, NEG_SCORE))
    score = jnp.where(even, score, -3e38)
    rank = jnp.zeros((r, LANES), jnp.int32)
    for i in range(LANES // 2):
        sj = score[:, 2 * i:2 * i + 1]
        ge = jnp.where(sj >= score, 1, 0)
        gt = jnp.where(sj > score, 1, 0)
        rank = rank + jnp.where(lane > 2 * i, ge, gt)
    return (rank < N_SELECT) & even


def _sel_expand_matrix(key0, width):
    r = lax.broadcasted_iota(jnp.int32, (LANES, width), 0)
    c = lax.broadcasted_iota(jnp.int32, (LANES, width), 1)
    hit = ((r & 1) == 0) & ((r >> 1) == ((key0 + c) >> SEL_SHIFT))
    return jnp.where(hit, 1.0, 0.0).astype(BF16)


def _rope_compress_kernel(q_ref, kvc_ref, kvs_ref, kvw_ref, cos_ref, sin_ref, pk_ref, pv_ref,
                          qr_ref, kc_ref, ks_ref, kw_ref, kcmp_ref, vcmp_ref):
    cosf, sinf = cos_ref[...], sin_ref[...]
    tq = q_ref.shape[0]
    for h in range(N_HEADS):
        sl = slice(h * HEAD_DIM, (h + 1) * HEAD_DIM)
        qr_ref[:, sl] = _rope(q_ref[:, sl], cosf, sinf).astype(qr_ref.dtype)
    for src, dst in ((kvc_ref, kc_ref), (kvs_ref, ks_ref), (kvw_ref, kw_ref)):
        for g in range(KV_HEADS):
            sl = slice(g * HEAD_DIM, (g + 1) * HEAD_DIM)
            dst[:, sl] = _rope(src[:, sl], cosf, sinf)
    nb = tq // CMP_BLOCK
    kc = kc_ref[...].reshape(nb, CMP_BLOCK, KV_WIDTH)
    vc = kvc_ref[:, KV_WIDTH:2 * KV_WIDTH].reshape(nb, CMP_BLOCK, KV_WIDTH)
    kcmp_ref[...] = jnp.sum(kc * pk_ref[...][None], axis=1)
    vcmp_ref[...] = jnp.sum(vc * pv_ref[...][None], axis=1)


def rope_compress(proj, cosf, sinf, pool_k, pool_v, b, t):
    tq = 256
    nq = t // tq
    m = b * t
    kvb = C_KV // (2 * KV_WIDTH)
    return pl.pallas_call(
        _rope_compress_kernel,
        out_shape=(jax.ShapeDtypeStruct((m, ATTN_WIDTH), BF16),
                   jax.ShapeDtypeStruct((m, KV_WIDTH), F32), jax.ShapeDtypeStruct((m, KV_WIDTH), F32),
                   jax.ShapeDtypeStruct((m, KV_WIDTH), F32),
                   jax.ShapeDtypeStruct((m // CMP_BLOCK, KV_WIDTH), F32),
                   jax.ShapeDtypeStruct((m // CMP_BLOCK, KV_WIDTH), F32)),
        grid=(b, nq),
        in_specs=[pl.BlockSpec((tq, ATTN_WIDTH), lambda bi, i: (bi * nq + i, C_Q // ATTN_WIDTH)),
                  pl.BlockSpec((tq, 2 * KV_WIDTH), lambda bi, i: (bi * nq + i, kvb)),
                  pl.BlockSpec((tq, 2 * KV_WIDTH), lambda bi, i: (bi * nq + i, kvb + 1)),
                  pl.BlockSpec((tq, 2 * KV_WIDTH), lambda bi, i: (bi * nq + i, kvb + 2)),
                  pl.BlockSpec((tq, HEAD_DIM), lambda bi, i: (i, 0)),
                  pl.BlockSpec((tq, HEAD_DIM), lambda bi, i: (i, 0)),
                  pl.BlockSpec((CMP_BLOCK, KV_WIDTH), lambda bi, i: (0, 0)),
                  pl.BlockSpec((CMP_BLOCK, KV_WIDTH), lambda bi, i: (0, 0))],
        out_specs=(pl.BlockSpec((tq, ATTN_WIDTH), lambda bi, i: (bi * nq + i, 0)),
                   pl.BlockSpec((tq, KV_WIDTH), lambda bi, i: (bi * nq + i, 0)),
                   pl.BlockSpec((tq, KV_WIDTH), lambda bi, i: (bi * nq + i, 0)),
                   pl.BlockSpec((tq, KV_WIDTH), lambda bi, i: (bi * nq + i, 0)),
                   pl.BlockSpec((tq // CMP_BLOCK, KV_WIDTH), lambda bi, i: (bi * nq + i, 0)),
                   pl.BlockSpec((tq // CMP_BLOCK, KV_WIDTH), lambda bi, i: (bi * nq + i, 0))),
        compiler_params=_cparams("parallel", "parallel"),
        name="rope_compress",
    )(proj, proj, proj, proj, cosf, sinf, pool_k, pool_v)


def _cmp_select_kernel(q_ref, kc_ref, vc_ref, oc_ref, sel_ref):
    i = pl.program_id(1)
    tq = q_ref.shape[0]
    qpos = i * tq + lax.broadcasted_iota(jnp.int32, (tq, 1), 0)
    lane = lax.broadcasted_iota(jnp.int32, (tq, LANES), 1)
    vis = (lane + 1) * CMP_BLOCK - 1 <= qpos
    for g in range(KV_HEADS):
        gs = slice(g * HEAD_DIM, (g + 1) * HEAD_DIM)
        kc = kc_ref[:, gs].astype(BF16)
        vc = vc_ref[:, gs].astype(BF16)
        imp = jnp.zeros((tq, LANES), F32)
        for j in range(Q_PER_KV):
            hs = slice((g * Q_PER_KV + j) * HEAD_DIM, (g * Q_PER_KV + j + 1) * HEAD_DIM)
            p = _masked_softmax(_dot_nt(q_ref[:, hs], kc) * ATTN_SCALE, vis)
            oc_ref[:, hs] = _dot(p.astype(BF16), vc)
            imp = imp + p
        sel_ref[g] = jnp.where(_select_blocks(imp, qpos), 1.0, 0.0).astype(sel_ref.dtype)


def cmp_attend_select(q_r, kcmp, vcmp, b, t):
    tq = 128
    nq = t // tq
    n = t // CMP_BLOCK
    assert n == LANES
    return pl.pallas_call(
        _cmp_select_kernel,
        out_shape=(jax.ShapeDtypeStruct((b * t, ATTN_WIDTH), F32),
                   jax.ShapeDtypeStruct((b, KV_HEADS, t, LANES), BF16)),
        grid=(b, nq),
        in_specs=[pl.BlockSpec((tq, ATTN_WIDTH), lambda bi, i: (bi * nq + i, 0)),
                  pl.BlockSpec((n, KV_WIDTH), lambda bi, i: (bi, 0)),
                  pl.BlockSpec((n, KV_WIDTH), lambda bi, i: (bi, 0))],
        out_specs=(pl.BlockSpec((tq, ATTN_WIDTH), lambda bi, i: (bi * nq + i, 0)),
                   pl.BlockSpec((None, KV_HEADS, tq, LANES), lambda bi, i: (bi, 0, i, 0))),
        compiler_params=_cparams("parallel", "parallel"),
        name="cmp_attend_select",
    )(q_r, kcmp, vcmp)


def _sel_attend_kernel(q_ref, k_ref, v_ref, sel_ref, o_ref, m_scr, l_scr, acc_scr, *, tq, tk):
    qi = pl.program_id(2)
    ki = pl.program_id(3)
    nk = pl.num_programs(3)

    @pl.when(ki == 0)
    def _():
        m_scr[...] = jnp.full_like(m_scr, NEG_SCORE)
        l_scr[...] = jnp.zeros_like(l_scr)
        acc_scr[...] = jnp.zeros_like(acc_scr)

    @pl.when(ki * tk <= qi * tq + tq - 1)
    def _():
        k = k_ref[...].astype(BF16)
        v = v_ref[...].astype(BF16)
        picked = _dot(sel_ref[...], _sel_expand_matrix(ki * tk, tk))
        qpos = qi * tq + lax.broadcasted_iota(jnp.int32, (tq, tk), 0)
        kpos = ki * tk + lax.broadcasted_iota(jnp.int32, (tq, tk), 1)
        mask = (picked > 0.5) & (kpos <= qpos)
        for j in range(Q_PER_KV):
            hs = slice(j * HEAD_DIM, (j + 1) * HEAD_DIM)
            s = jnp.where(mask, _dot_nt(q_ref[:, hs], k) * ATTN_SCALE, NEG_SCORE)
            m_old = m_scr[j]
            m_new = jnp.maximum(m_old, jnp.max(s, axis=-1, keepdims=True))
            p = jnp.where(mask, jnp.exp(s - m_new), 0.0)
            alpha = jnp.exp(m_old - m_new)
            l_scr[j] = alpha * l_scr[j] + jnp.sum(p, axis=-1, keepdims=True)
            acc_scr[j] = alpha * acc_scr[j] + _dot(p.astype(BF16), v)
            m_scr[j] = m_new

    @pl.when(ki == nk - 1)
    def _():
        for j in range(Q_PER_KV):
            o_ref[:, j * HEAD_DIM:(j + 1) * HEAD_DIM] = acc_scr[j] / jnp.maximum(l_scr[j], 1e-30)


def sel_attend(q_r, ks_r, proj, sel, b, t):
    tq, tk = 256, 512
    nq, nk = t // tq, t // tk
    vs_blk = (C_KV + 3 * KV_WIDTH) // HEAD_DIM
    last = lambda qi: (qi * tq + tq - 1) // tk
    return pl.pallas_call(
        functools.partial(_sel_attend_kernel, tq=tq, tk=tk),
        out_shape=jax.ShapeDtypeStruct((b * t, ATTN_WIDTH), F32),
        grid=(b, KV_HEADS, nq, nk),
        in_specs=[pl.BlockSpec((tq, Q_PER_KV * HEAD_DIM), lambda bi, g, qi, ki: (bi * nq + qi, g)),
                  pl.BlockSpec((tk, HEAD_DIM), lambda bi, g, qi, ki: (bi * nk + jnp.minimum(ki, last(qi)), g)),
                  pl.BlockSpec((tk, HEAD_DIM),
                               lambda bi, g, qi, ki: (bi * nk + jnp.minimum(ki, last(qi)), vs_blk + g)),
                  pl.BlockSpec((None, None, tq, LANES), lambda bi, g, qi, ki: (bi, g, qi, 0))],
        out_specs=pl.BlockSpec((tq, Q_PER_KV * HEAD_DIM), lambda bi, g, qi, ki: (bi * nq + qi, g)),
        scratch_shapes=[pltpu.VMEM((Q_PER_KV, tq, 1), F32), pltpu.VMEM((Q_PER_KV, tq, 1), F32),
                        pltpu.VMEM((Q_PER_KV, tq, HEAD_DIM), F32)],
        compiler_params=_cparams("parallel", "parallel", "parallel", "arbitrary"),
        name="sel_attend",
    )(q_r, ks_r, proj, sel)


def _win_combine_kernel(q_ref, kp_ref, kc_ref, vp_ref, vc_ref, oc_ref, os_ref, misc_ref, o_ref, *, tq):
    g = pl.program_id(1)
    qi = pl.program_id(2)
    k = jnp.concatenate([kp_ref[...], kc_ref[...]], axis=0).astype(BF16)
    v = jnp.concatenate([vp_ref[...], vc_ref[...]], axis=0).astype(BF16)
    qpos = qi * tq + lax.broadcasted_iota(jnp.int32, (tq, 2 * tq), 0)
    kpos = (qi - 1) * tq + lax.broadcasted_iota(jnp.int32, (tq, 2 * tq), 1)
    diff = qpos - kpos
    mask = (diff >= 0) & (diff <= WINDOW) & (kpos >= 0)
    gates = _sigmoid(misc_ref[...])
    lane = lax.broadcasted_iota(jnp.int32, (tq, LANES), 1)
    for j in range(Q_PER_KV):
        hs = slice(j * HEAD_DIM, (j + 1) * HEAD_DIM)
        p = _masked_softmax(_dot_nt(q_ref[:, hs], k) * ATTN_SCALE, mask)
        ow = _dot(p.astype(BF16), v)
        base = GN_OFF + 3 * (g * Q_PER_KV + j)
        gc = jnp.sum(jnp.where(lane == base, gates, 0.0), axis=-1, keepdims=True)
        gsel = jnp.sum(jnp.where(lane == base + 1, gates, 0.0), axis=-1, keepdims=True)
        gw = jnp.sum(jnp.where(lane == base + 2, gates, 0.0), axis=-1, keepdims=True)
        o_ref[:, hs] = (gc * oc_ref[:, hs] + gsel * os_ref[:, hs] + gw * ow).astype(o_ref.dtype)


def window_combine(q_r, kw_r, proj, o_c, o_s, b, t):
    tq = WINDOW
    nq = t // tq
    vw_blk = (C_KV + 5 * KV_WIDTH) // HEAD_DIM
    gw = Q_PER_KV * HEAD_DIM
    prev = lambda qi: jnp.maximum(qi - 1, 0)
    return pl.pallas_call(
        functools.partial(_win_combine_kernel, tq=tq),
        out_shape=jax.ShapeDtypeStruct((b * t, ATTN_WIDTH), BF16),
        grid=(b, KV_HEADS, nq),
        in_specs=[pl.BlockSpec((tq, gw), lambda bi, g, qi: (bi * nq + qi, g)),
                  pl.BlockSpec((tq, HEAD_DIM), lambda bi, g, qi: (bi * nq + prev(qi), g)),
                  pl.BlockSpec((tq, HEAD_DIM), lambda bi, g, qi: (bi * nq + qi, g)),
                  pl.BlockSpec((tq, HEAD_DIM), lambda bi, g, qi: (bi * nq + prev(qi), vw_blk + g)),
                  pl.BlockSpec((tq, HEAD_DIM), lambda bi, g, qi: (bi * nq + qi, vw_blk + g)),
                  pl.BlockSpec((tq, gw), lambda bi, g, qi: (bi * nq + qi, g)),
                  pl.BlockSpec((tq, gw), lambda bi, g, qi: (bi * nq + qi, g)),
                  pl.BlockSpec((tq, LANES), lambda bi, g, qi: (bi * nq + qi, C_MISC // LANES))],
        out_specs=pl.BlockSpec((tq, gw), lambda bi, g, qi: (bi * nq + qi, g)),
        compiler_params=_cparams("parallel", "parallel", "parallel"),
        name="window_combine",
    )(q_r, kw_r, kw_r, proj, proj, o_c, o_s, proj)


def _group_rows(fn):
    out = None
    rowgrp = None
    for g in range(KV_HEADS):
        r = fn(g)
        if out is None:
            rowgrp = lax.broadcasted_iota(jnp.int32, r.shape, 0) // Q_PER_KV
            out = r
        else:
            out = jnp.where(rowgrp == g, r, out)
    return out


def _per_head_rows(x4):
    return _group_rows(lambda g: jnp.broadcast_to(x4[g:g + 1, :], (N_HEADS, x4.shape[1])))


def _sample_pages_kernel(pt_ref, q_ref, kn_ref, cos_ref, sin_ref, pk_ref, pv_ref, *rest, n_pages, past):
    pages = rest[:4 * n_pages]
    oc_ref, os_ref, kr_ref = rest[4 * n_pages:4 * n_pages + 3]
    kc_scr, vc_scr, ks_scr, vs_scr = rest[4 * n_pages + 3:]
    kc_pages, vc_pages = pages[0:n_pages], pages[n_pages:2 * n_pages]
    ks_pages, vs_pages = pages[2 * n_pages:3 * n_pages], pages[3 * n_pages:4 * n_pages]
    cosf, sinf = cos_ref[...], sin_ref[...]
    nkeys = n_pages * PAGE_SIZE

    q = (_rope(q_ref[...], cosf, sinf)).astype(BF16)
    kn = kn_ref[...]
    kc_new = _rope(kn[0:KV_HEADS], cosf, sinf)
    ks_new = _rope(kn[2 * KV_HEADS:3 * KV_HEADS], cosf, sinf)
    kw_new = _rope(kn[4 * KV_HEADS:5 * KV_HEADS], cosf, sinf)
    vs_new = kn[3 * KV_HEADS:4 * KV_HEADS]
    kr_ref[0:KV_HEADS, :] = kc_new
    kr_ref[KV_HEADS:2 * KV_HEADS, :] = ks_new
    kr_ref[2 * KV_HEADS:3 * KV_HEADS, :] = kw_new

    per_page = PAGE_SIZE // CMP_BLOCK
    n_cmp = n_pages * per_page
    kc_scr[n_cmp:, :] = jnp.zeros((LANES - n_cmp, KV_WIDTH), F32)
    vc_scr[n_cmp:, :] = jnp.zeros((LANES - n_cmp, KV_WIDTH), F32)
    pk, pv = pk_ref[...][None], pv_ref[...][None]
    for j in range(n_pages):
        rows = slice(j * per_page, (j + 1) * per_page)
        kc_scr[rows, :] = jnp.sum(kc_pages[j][...].reshape(per_page, CMP_BLOCK, KV_WIDTH) * pk, axis=1)
        vc_scr[rows, :] = jnp.sum(vc_pages[j][...].reshape(per_page, CMP_BLOCK, KV_WIDTH) * pv, axis=1)
        ks_scr[j * PAGE_SIZE:(j + 1) * PAGE_SIZE, :] = ks_pages[j][...].astype(BF16)
        vs_scr[j * PAGE_SIZE:(j + 1) * PAGE_SIZE, :] = vs_pages[j][...].astype(BF16)

    qpos = jnp.full((N_HEADS, 1), past, jnp.int32)
    lane = lax.broadcasted_iota(jnp.int32, (N_HEADS, LANES), 1)
    gcol = lambda g: slice(g * HEAD_DIM, (g + 1) * HEAD_DIM)

    vis = (lane + 1) * CMP_BLOCK - 1 <= qpos
    s_c = _group_rows(lambda g: _dot_nt(q, kc_scr[:, gcol(g)].astype(BF16))) * ATTN_SCALE
    p_c = _masked_softmax(s_c, vis)
    p_cb = p_c.astype(BF16)
    oc_ref[...] = _group_rows(lambda g: _dot(p_cb, vc_scr[:, gcol(g)].astype(BF16)))
    ri = lax.broadcasted_iota(jnp.int32, (N_HEADS, N_HEADS), 0) // Q_PER_KV
    ci = lax.broadcasted_iota(jnp.int32, (N_HEADS, N_HEADS), 1) // Q_PER_KV
    imp = _dot_exact(jnp.where(ri == ci, 1.0, 0.0), p_c)
    sel = jnp.where(_select_blocks(imp, qpos), 1.0, 0.0).astype(BF16)

    picked = _dot(sel, _sel_expand_matrix(0, nkeys))
    kpos = lax.broadcasted_iota(jnp.int32, (N_HEADS, nkeys), 1)
    mask = (picked > 0.5) & (kpos <= qpos)
    s = jnp.where(mask, _group_rows(lambda g: _dot_nt(q, ks_scr[:, gcol(g)])) * ATTN_SCALE, NEG_SCORE)
    new_blk = past // SEL_BLOCK
    new_picked = jnp.sum(jnp.where(lane == 2 * new_blk, sel.astype(F32), 0.0), axis=-1, keepdims=True) > 0.5
    s_new = jnp.sum(q.astype(F32) * _per_head_rows(ks_new), axis=-1, keepdims=True)
    s_new = jnp.where(new_picked, s_new * ATTN_SCALE, NEG_SCORE)
    m = jnp.maximum(jnp.max(s, axis=-1, keepdims=True), s_new)
    e = jnp.where(mask, jnp.exp(s - m), 0.0)
    e_new = jnp.where(new_picked, jnp.exp(s_new - m), 0.0)
    denom = jnp.maximum(jnp.sum(e, axis=-1, keepdims=True) + e_new, 1e-30)
    eb = e.astype(BF16)
    acc = _group_rows(lambda g: _dot(eb, vs_scr[:, gcol(g)]))
    acc = acc + e_new * _per_head_rows(vs_new)
    os_ref[...] = acc / denom


def sample_pages_attend(q3, kn3, cos_row, sin_row, pool_k, pool_v, caches, page_table, layer, past):
    db, n_pages = page_table.shape
    assert n_pages * PAGE_SIZE == past and n_pages * (PAGE_SIZE // CMP_BLOCK) <= LANES

    def page_spec(j):
        return pl.BlockSpec((None, None, PAGE_SIZE, KV_WIDTH), lambda i, pt: (layer, pt[i, j], 0, 0))

    page_specs = [page_spec(j) for _ in range(4) for j in range(n_pages)]
    page_args = [c for c in caches for _ in range(n_pages)]
    grid_spec = pltpu.PrefetchScalarGridSpec(
        num_scalar_prefetch=1,
        grid=(db,),
        in_specs=[pl.BlockSpec((None, N_HEADS, HEAD_DIM), lambda i, pt: (i, 0, 0)),
                  pl.BlockSpec((None, 6 * KV_HEADS, HEAD_DIM), lambda i, pt: (i, 0, 0)),
                  pl.BlockSpec((1, HEAD_DIM), lambda i, pt: (0, 0)),
                  pl.BlockSpec((1, HEAD_DIM), lambda i, pt: (0, 0)),
                  pl.BlockSpec((CMP_BLOCK, KV_WIDTH), lambda i, pt: (0, 0)),
                  pl.BlockSpec((CMP_BLOCK, KV_WIDTH), lambda i, pt: (0, 0))] + page_specs,
        out_specs=(pl.BlockSpec((None, N_HEADS, HEAD_DIM), lambda i, pt: (i, 0, 0)),
                   pl.BlockSpec((None, N_HEADS, HEAD_DIM), lambda i, pt: (i, 0, 0)),
                   pl.BlockSpec((None, 3 * KV_HEADS, HEAD_DIM), lambda i, pt: (i, 0, 0))),
        scratch_shapes=[pltpu.VMEM((LANES, KV_WIDTH), F32), pltpu.VMEM((LANES, KV_WIDTH), F32),
                        pltpu.VMEM((past, KV_WIDTH), BF16), pltpu.VMEM((past, KV_WIDTH), BF16)],
    )
    return pl.pallas_call(
        functools.partial(_sample_pages_kernel, n_pages=n_pages, past=past),
        out_shape=(jax.ShapeDtypeStruct((db, N_HEADS, HEAD_DIM), F32),
                   jax.ShapeDtypeStruct((db, N_HEADS, HEAD_DIM), F32),
                   jax.ShapeDtypeStruct((db, 3 * KV_HEADS, HEAD_DIM), F32)),
        grid_spec=grid_spec,
        compiler_params=_cparams("arbitrary"),
        name="sample_pages_attend",
    )(page_table, q3, kn3, cos_row, sin_row, pool_k, pool_v, *page_args)


def _sample_window_kernel(q_ref, kr_ref, kn_ref, cos_ref, sin_ref, kwin_ref, vwin_ref, oc_ref, os_ref, gn_ref,
                          o_ref, kout_ref, vout_ref, *, past):
    wb = kwin_ref.shape[0]
    q = _rope(q_ref[...], cos_ref[...], sin_ref[...]).astype(BF16)
    kw_new = kr_ref[2 * KV_HEADS:3 * KV_HEADS, :]
    vw_new = kn_ref[5 * KV_HEADS:6 * KV_HEADS, :]
    kwin = kwin_ref[...]
    vwin = vwin_ref[...]
    kb, vb = kwin.astype(BF16), vwin.astype(BF16)
    gcol = lambda g: slice(g * HEAD_DIM, (g + 1) * HEAD_DIM)

    qpos = past
    kpos = past - wb + lax.broadcasted_iota(jnp.int32, (N_HEADS, wb), 1)
    diff = qpos - kpos
    mask = (diff >= 0) & (diff <= WINDOW)
    s = jnp.where(mask, _group_rows(lambda g: _dot_nt(q, kb[:, gcol(g)])) * ATTN_SCALE, NEG_SCORE)
    s_new = jnp.sum(q.astype(F32) * _per_head_rows(kw_new), axis=-1, keepdims=True) * ATTN_SCALE
    m = jnp.maximum(jnp.max(s, axis=-1, keepdims=True), s_new)
    e = jnp.where(mask, jnp.exp(s - m), 0.0)
    e_new = jnp.exp(s_new - m)
    denom = jnp.maximum(jnp.sum(e, axis=-1, keepdims=True) + e_new, 1e-30)
    eb = e.astype(BF16)
    acc = _group_rows(lambda g: _dot(eb, vb[:, gcol(g)]))
    acc = acc + e_new * _per_head_rows(vw_new)
    ow = acc / denom

    gates = _sigmoid(gn_ref[...])
    o_ref[...] = gates[:, 0:1] * oc_ref[...] + gates[:, 1:2] * os_ref[...] + gates[:, 2:3] * ow

    row = lax.broadcasted_iota(jnp.int32, (wb, KV_WIDTH), 0)
    k_row = jnp.concatenate([kw_new[g:g + 1, :] for g in range(KV_HEADS)], axis=1)
    v_row = jnp.concatenate([vw_new[g:g + 1, :] for g in range(KV_HEADS)], axis=1)
    kout_ref[...] = jnp.where(row == wb - 1, k_row, pltpu.roll(kwin, wb - 1, 0))
    vout_ref[...] = jnp.where(row == wb - 1, v_row, pltpu.roll(vwin, wb - 1, 0))


def sample_window_combine(q3, kr3, kn3, cos_row, sin_row, kwin, vwin, o_c, o_s, gn3, layer, past):
    db = q3.shape[0]
    wb = kwin.shape[2]
    head = lambda rows: pl.BlockSpec((None, rows, HEAD_DIM), lambda i: (i, 0, 0))
    return pl.pallas_call(
        functools.partial(_sample_window_kernel, past=past),
        out_shape=(jax.ShapeDtypeStruct((db, N_HEADS, HEAD_DIM), F32),
                   jax.ShapeDtypeStruct((db, wb, KV_WIDTH), F32), jax.ShapeDtypeStruct((db, wb, KV_WIDTH), F32)),
        grid=(db,),
        in_specs=[head(N_HEADS), head(3 * KV_HEADS), head(6 * KV_HEADS),
                  pl.BlockSpec((1, HEAD_DIM), lambda i: (0, 0)), pl.BlockSpec((1, HEAD_DIM), lambda i: (0, 0)),
                  pl.BlockSpec((None, None, wb, KV_WIDTH), lambda i: (layer, i, 0, 0)),
                  pl.BlockSpec((None, None, wb, KV_WIDTH), lambda i: (layer, i, 0, 0)),
                  head(N_HEADS), head(N_HEADS),
                  pl.BlockSpec((None, N_HEADS, 3), lambda i: (i, 0, 0))],
        out_specs=(head(N_HEADS),
                   pl.BlockSpec((None, wb, KV_WIDTH), lambda i: (i, 0, 0)),
                   pl.BlockSpec((None, wb, KV_WIDTH), lambda i: (i, 0, 0))),
        compiler_params=_cparams("parallel"),
        name="sample_window_combine",
    )(q3, kr3, kn3, cos_row, sin_row, kwin, vwin, o_c, o_s, gn3)


def _reorder_w_in(w_in):
    o_z, o_xbc = 0, D_INNER
    o_dt = o_xbc + CONV_DIM
    o_q = o_dt + SSM_HEADS
    o_kv = o_q + ATTN_WIDTH
    o_gn = o_kv + 6 * KV_WIDTH
    o_gm = o_gn + 3 * N_HEADS
    sl = lambda a, n: w_in[..., a:a + n]
    pad = jnp.zeros(w_in.shape[:-1] + (MISC_WIDTH - SSM_HEADS - 3 * N_HEADS,), w_in.dtype)
    parts = [sl(o_xbc, D_INNER), sl(o_z, D_INNER), sl(o_gm, 2 * D_MODEL), sl(o_xbc + D_INNER, BC_WIDTH),
             sl(o_q, ATTN_WIDTH), sl(o_kv, 6 * KV_WIDTH), sl(o_dt, SSM_HEADS), sl(o_gn, 3 * N_HEADS), pad]
    return jnp.concatenate(parts, axis=-1).astype(BF16)


def _rope_tables(pos):
    half = HEAD_DIM // 2
    inv = ROPE_THETA ** (-jnp.arange(half, dtype=F32) / half)
    ang = pos.astype(F32)[:, None] * inv[None, :]
    cos, sin = jnp.cos(ang), jnp.sin(ang)
    return jnp.concatenate([cos, cos], axis=-1), jnp.concatenate([-sin, sin], axis=-1)


def _pad_lanes(v):
    return jnp.pad(v, (0, LANES - v.shape[0])).reshape(1, LANES)


def kernel(x_prompt, x_sample, cache_k_cmp, cache_v_cmp, cache_k_sel, cache_v_sel, cache_k_win, cache_v_win,
           state_ssm, state_conv, page_table, c_prompt, c_sample, w_ada, b_ada, g_norm_mix, g_norm_ffn,
           g_norm_final, w_in, conv_w, conv_b, dt_bias, a_log, d_skip, g_ssm, pool_k, pool_v, w_ssm_br,
           w_attn_br, w_out, w_ffn_in, w_ffn_out):
    b, t, _ = x_prompt.shape
    db = x_sample.shape[0]
    past = page_table.shape[1] * PAGE_SIZE
    wb = cache_k_win.shape[2]
    assert x_sample.shape[1] == 1 and t % WINDOW == 0 and wb == WINDOW

    w_in_r = _reorder_w_in(w_in)
    w_ssm_b, w_attn_b, w_out_b = w_ssm_br.astype(BF16), w_attn_br.astype(BF16), w_out.astype(BF16)
    w_f1_b, w_f2_b = w_ffn_in.astype(BF16), w_ffn_out.astype(BF16)
    caches = [c.reshape(DEPTH, c.shape[1], PAGE_SIZE, KV_WIDTH)
              for c in (cache_k_cmp, cache_v_cmp, cache_k_sel, cache_v_sel)]
    kwin_all = cache_k_win.reshape(DEPTH, db, wb, KV_WIDTH)
    vwin_all = cache_v_win.reshape(DEPTH, db, wb, KV_WIDTH)
    cos_p, sin_p = _rope_tables(jnp.arange(t, dtype=jnp.int32))
    cos_s, sin_s = _rope_tables(jnp.full((1,), past, jnp.int32))

    n_c = b + db
    n_c_pad = -(-n_c // SUBLANES) * SUBLANES
    c_all = jnp.pad(jnp.concatenate([c_prompt, c_sample], axis=0), ((0, n_c_pad - n_c), (0, 0)))
    mod_all = ada_modulation(c_all, w_ada, b_ada)

    xp = x_prompt.reshape(b * t, D_MODEL)
    xs = x_sample.reshape(db, D_MODEL)
    tm_p = 512
    st_p, st_s = [], []
    for l in range(DEPTH):
        mod_p = mod_all[l, :b].reshape(b, 1, 6 * D_MODEL)
        mod_s = mod_all[l, b:n_c]
        cwx, cwbc = conv_w[l][:, :D_INNER], conv_w[l][:, D_INNER:]
        cbx, cbbc = conv_b[l][None, :D_INNER], conv_b[l][None, D_INNER:]
        dtb, alog = _pad_lanes(dt_bias[l]), _pad_lanes(a_log[l])
        dskip = jnp.repeat(d_skip[l], SSM_HEAD_DIM)[None, :]
        gssm = g_ssm[l][None, :]
        pk, pv = pool_k[l].reshape(CMP_BLOCK, KV_WIDTH), pool_v[l].reshape(CMP_BLOCK, KV_WIDTH)

        h = norm_modulate(xp, g_norm_mix[l], mod_p, 1, 0, t, tm_p)
        proj = in_projection(h, w_in_r[l], tm_p)
        y_ssm, ssm_fin, tailx, tailbc = ssd_prompt(proj, b, t, cwx, cbx, cwbc, cbbc, dtb, alog, dskip, gssm)
        q_r, kc_r, ks_r, kw_r, kcmp, vcmp = rope_compress(proj, cos_p, sin_p, pk, pv, b, t)
        o_c, sel = cmp_attend_select(q_r, kcmp, vcmp, b, t)
        o_s = sel_attend(q_r, ks_r, proj, sel, b, t)
        o_attn = window_combine(q_r, kw_r, proj, o_c, o_s, b, t)
        merged = merge_branches(y_ssm, o_attn, w_ssm_b[l], w_attn_b[l], proj, tm_p)
        xp = residual_matmul(merged, w_out_b[l], xp, mod_p, 2, t, tm_p, 1024)
        h2 = norm_modulate(xp, g_norm_ffn[l], mod_p, 4, 3, t, tm_p)
        act = glu_matmul(h2, w_f1_b[l], tm_p)
        xp = residual_matmul(act, w_f2_b[l], xp, mod_p, 5, t, tm_p, 512)

        kv = lambda a: a.reshape(b, t, KV_HEADS, HEAD_DIM)
        pcol = lambda i: proj[:, C_KV + i * KV_WIDTH:C_KV + (i + 1) * KV_WIDTH]
        conv_tail = jnp.concatenate([tailx, tailbc], axis=-1)[:, SUBLANES - (CONV_WIDTH - 1):]
        st_p.append((kv(kc_r), kv(pcol(1)), kv(ks_r), kv(pcol(3)), kv(kw_r)[:, t - wb:], kv(pcol(5))[:, t - wb:],
                     ssm_fin.reshape(b, SSM_HEADS, SSM_HEAD_DIM, SSM_STATE), conv_tail))

        hs = norm_modulate(xs, g_norm_mix[l], mod_s, 1, 0, 0, db)
        projs = in_projection(hs, w_in_r[l], db)
        st_t = jnp.transpose(state_conv[l], (1, 0, 2))
        xact, bcact, dts, das = ssd_sample_pre(projs, st_t[:, :, :D_INNER], st_t[:, :, D_INNER:],
                                               cwx, cbx, cwbc, cbbc, dtb, alog)
        xt = jnp.transpose(xact.reshape(db, SSM_HEADS, SSM_HEAD_DIM), (0, 2, 1))
        ssm_new, yt = ssd_sample_step(xt, dts[:, None, :SSM_HEADS], das[:, None, :SSM_HEADS],
                                      bcact[:, :SSM_GROUPS * SSM_STATE].reshape(db, SSM_GROUPS, SSM_STATE),
                                      bcact[:, SSM_GROUPS * SSM_STATE:].reshape(db, SSM_GROUPS, SSM_STATE),
                                      state_ssm, l)
        y_s = jnp.transpose(yt, (0, 2, 1)).reshape(db, D_INNER)
        y_ssm_s = ssd_sample_finish(y_s, xact, projs, dskip, gssm)

        q3 = projs[:, C_Q:C_Q + ATTN_WIDTH].reshape(db, N_HEADS, HEAD_DIM)
        kn3 = projs[:, C_KV:C_KV + 6 * KV_WIDTH].reshape(db, 6 * KV_HEADS, HEAD_DIM)
        gn3 = projs[:, C_MISC + GN_OFF:C_MISC + GN_OFF + 3 * N_HEADS].reshape(db, N_HEADS, 3)
        oc_s, os_s, kr3 = sample_pages_attend(q3, kn3, cos_s, sin_s, pk, pv, caches, page_table, l, past)
        o_s3, kwin_new, vwin_new = sample_window_combine(q3, kr3, kn3, cos_s, sin_s, kwin_all, vwin_all,
                                                         oc_s, os_s, gn3, l, past)
        o_attn_s = o_s3.reshape(db, ATTN_WIDTH).astype(BF16)
        merged_s = merge_branches(y_ssm_s, o_attn_s, w_ssm_b[l], w_attn_b[l], projs, db)
        xs = residual_matmul(merged_s, w_out_b[l], xs, mod_s, 2, 0, db, 1024)
        h2s = norm_modulate(xs, g_norm_ffn[l], mod_s, 4, 3, 0, db)
        act_s = glu_matmul(h2s, w_f1_b[l], db)
        xs = residual_matmul(act_s, w_f2_b[l], xs, mod_s, 5, 0, db, 512)

        kvs = lambda a: a.reshape(db, 1, KV_HEADS, HEAD_DIM)
        xbc_raw = jnp.concatenate([projs[:, C_X:C_X + D_INNER], projs[:, C_BC:C_BC + BC_WIDTH]], axis=-1)
        conv_new = jnp.concatenate([state_conv[l][:, 1:], xbc_raw[:, None, :]], axis=1)
        st_s.append((kvs(kr3[:, 0:KV_HEADS]), kvs(kn3[:, KV_HEADS:2 * KV_HEADS]),
                     kvs(kr3[:, KV_HEADS:2 * KV_HEADS]), kvs(kn3[:, 3 * KV_HEADS:4 * KV_HEADS]),
                     kwin_new.reshape(db, wb, KV_HEADS, HEAD_DIM), vwin_new.reshape(db, wb, KV_HEADS, HEAD_DIM),
                     ssm_new, conv_new))

    y_prompt = final_norm(xp, g_norm_final, tm_p).reshape(b, t, D_MODEL)
    y_sample = final_norm(xs, g_norm_final, db).reshape(db, 1, D_MODEL)
    outs_p = [jnp.stack([s[i] for s in st_p]) for i in range(8)]
    outs_s = [jnp.stack([s[i] for s in st_s]) for i in range(8)]
    return (y_prompt, y_sample, *outs_p, *outs_s)
```
